```python
import jax, jax.numpy as jnp
from jax import lax
import numpy as np


D_MODEL = 2048
BATCH = 1
SEQ = 16384
DEPTH = 1
DEC_BATCH = 8
DEC_SEQ = 2048
PAST_LEN = 128

D_MIX = D_MODEL
D_SC = D_MIX // 2
SC_CONV = 3
D_SSM = D_MIX - D_SC
SSM_HEAD_DIM = 64
SSM_HEADS = D_SSM // SSM_HEAD_DIM
SSM_GROUPS = 4
D_STATE = 128
SSM_CONV = 5
CHUNK = 128
D_XBC = D_SSM + 2 * SSM_GROUPS * D_STATE
D_IN = 3 * D_SC + D_SSM + D_XBC + 2 * SSM_HEADS
D_FF = ((8 * D_MODEL // 3 + 255) // 256) * 256
EPS = 1e-6

kernel_name = 'hybrid_shortconv_ssd_encoder'


def rmsnorm(x, w):
    xf = x.astype(jnp.float32)
    y = xf * lax.rsqrt(jnp.mean(xf * xf, axis=-1, keepdims=True) + EPS)
    return (y * w.astype(jnp.float32)).astype(x.dtype)


def modulate(h, shift, scale):
    return h * (1.0 + scale[:, None, :]) + shift[:, None, :]


def dwconv_centered(u, w):
    k = w.shape[0]
    p = (k - 1) // 2
    l = u.shape[1]
    up = jnp.pad(u, ((0, 0), (p, p), (0, 0)))
    out = up[:, 0:l] * w[0]
    for i in range(1, k):
        out = out + up[:, i:i + l] * w[i]
    return out


def ssd_chunked(x, dt, a, bm, cm):
    b, l, h, p = x.shape
    g, n = bm.shape[2], bm.shape[3]
    r = h // g
    c = l // CHUNK
    xdt = (x * dt[..., None]).reshape(b, c, CHUNK, g, r, p)
    bc = bm.reshape(b, c, CHUNK, g, n)
    cc = cm.reshape(b, c, CHUNK, g, n)
    a_cum = jnp.cumsum((dt * a).reshape(b, c, CHUNK, g, r), axis=2)
    mask = jnp.tril(jnp.ones((CHUNK, CHUNK), dtype=bool))[None, None, :, :, None, None]
    seg = a_cum[:, :, :, None] - a_cum[:, :, None, :]
    lmat = jnp.where(mask, jnp.exp(jnp.where(mask, seg, 0.0)), 0.0)
    cb = jnp.einsum('bclgn,bcsgn->bclsg', cc, bc)
    y_diag = jnp.einsum('bclsg,bclsgr,bcsgrp->bclgrp', cb, lmat, xdt)
    decay_to_end = jnp.exp(a_cum[:, :, -1:] - a_cum)
    states = jnp.einsum('bcsgn,bcsgr,bcsgrp->bcgrpn', bc, decay_to_end, xdt)
    chunk_decay = jnp.exp(a_cum[:, :, -1])

    def step(carry, inp):
        s_c, d_c = inp
        new = d_c[..., None, None] * carry + s_c
        return new, carry

    init = jnp.zeros((b, g, r, p, n), dtype=jnp.float32)
    _, prev = lax.scan(step, init, (jnp.moveaxis(states, 1, 0), jnp.moveaxis(chunk_decay, 1, 0)))
    prev = jnp.moveaxis(prev, 0, 1)
    y_off = jnp.einsum('bclgn,bcgrpn,bclgr->bclgrp', cc, prev, jnp.exp(a_cum))
    return (y_diag + y_off).reshape(b, l, h, p)


def mixer(h, w_in, sc_conv_w, ssm_conv_w, ssm_conv_b, dt_bias_f, dt_bias_b,
          a_log_f, a_log_b, d_skip, ssm_norm_w, w_out):
    b, l, _ = h.shape
    proj = h @ w_in
    idx = [D_SC, 2 * D_SC, 3 * D_SC, 3 * D_SC + D_SSM, 3 * D_SC + D_SSM + D_XBC]
    sc_b, sc_c, sc_x, z, xbc, dt_raw = jnp.split(proj, idx, axis=-1)
    y_a = sc_b * dwconv_centered(sc_c * sc_x, sc_conv_w)
    xbc = jax.nn.silu(dwconv_centered(xbc, ssm_conv_w) + ssm_conv_b)
    xs, bs, cs = jnp.split(xbc, [D_SSM, D_SSM + SSM_GROUPS * D_STATE], axis=-1)
    xs = xs.astype(jnp.float32).reshape(b, l, SSM_HEADS, SSM_HEAD_DIM)
    bs = bs.astype(jnp.float32).reshape(b, l, SSM_GROUPS, D_STATE)
    cs = cs.astype(jnp.float32).reshape(b, l, SSM_GROUPS, D_STATE)
    dt_raw = dt_raw.astype(jnp.float32)
    dt_f = jax.nn.softplus(dt_raw[..., :SSM_HEADS] + dt_bias_f.astype(jnp.float32))
    dt_b = jax.nn.softplus(dt_raw[..., SSM_HEADS:] + dt_bias_b.astype(jnp.float32))
    a_f = -jnp.exp(a_log_f.astype(jnp.float32))
    a_b = -jnp.exp(a_log_b.astype(jnp.float32))
    y_f = ssd_chunked(xs, dt_f, a_f, bs, cs)
    y_bw = jnp.flip(ssd_chunked(jnp.flip(xs, 1), jnp.flip(dt_b, 1), a_b,
                                jnp.flip(bs, 1), jnp.flip(cs, 1)), 1)
    y = y_f + y_bw + d_skip.astype(jnp.float32)[:, None] * xs
    y = y.reshape(b, l, D_SSM) * jax.nn.silu(z.astype(jnp.float32))
    yg = y.reshape(b, l, SSM_GROUPS, D_SSM // SSM_GROUPS)
    yg = yg * lax.rsqrt(jnp.mean(yg * yg, axis=-1, keepdims=True) + EPS)
    y_b = (yg.reshape(b, l, D_SSM) * ssm_norm_w.astype(jnp.float32)).astype(h.dtype)
    return jnp.concatenate([y_a, y_b], axis=-1) @ w_out


def encoder_layer(x, c, w_ada, b_ada, norm_pre_mix, norm_post_mix, w_in, sc_conv_w,
                  ssm_conv_w, ssm_conv_b, dt_bias_f, dt_bias_b, a_log_f, a_log_b, d_skip,
                  ssm_norm_w, w_out, norm_pre_ffn, norm_post_ffn, w_gate, w_up, w_down):
    mod = jax.nn.silu(c) @ w_ada + b_ada
    sh1, sc1, g1, sh2, sc2, g2 = jnp.split(mod, 6, axis=-1)
    h = modulate(rmsnorm(x, norm_pre_mix), sh1, sc1)
    m = mixer(h, w_in, sc_conv_w, ssm_conv_w, ssm_conv_b, dt_bias_f, dt_bias_b,
              a_log_f, a_log_b, d_skip, ssm_norm_w, w_out)
    x = x + g1[:, None, :] * rmsnorm(m, norm_post_mix)
    h = modulate(rmsnorm(x, norm_pre_ffn), sh2, sc2)
    f = (jax.nn.silu(h @ w_gate) * (h @ w_up)) @ w_down
    return x + g2[:, None, :] * rmsnorm(f, norm_post_ffn)


def setup_inputs(seed: int = 0) -> dict:
    key = jax.random.key(seed)
    ks = jax.random.split(key, 24)
    f32 = jnp.float32
    s = D_MODEL ** -0.5

    def nrm(k, shape, scale):
        return jax.random.normal(k, shape, dtype=f32) * scale

    def gain(k, shape):
        return 1.0 + 0.02 * jax.random.normal(k, shape, dtype=f32)

    dt_f = jnp.exp(jax.random.uniform(ks[12], (DEPTH, SSM_HEADS), minval=np.log(1e-3), maxval=np.log(1e-1)))
    dt_b = jnp.exp(jax.random.uniform(ks[13], (DEPTH, SSM_HEADS), minval=np.log(1e-3), maxval=np.log(1e-1)))
    return {
        'x_prompt': nrm(ks[0], (BATCH, SEQ, D_MODEL), 1.0),
        'x_sample': nrm(ks[1], (DEC_BATCH, DEC_SEQ, D_MODEL), 1.0),
        'c_prompt': nrm(ks[2], (BATCH, D_MODEL), 1.0),
        'c_sample': nrm(ks[3], (DEC_BATCH, D_MODEL), 1.0),
        'w_ada': nrm(ks[4], (DEPTH, D_MODEL, 6 * D_MODEL), 0.5 * s),
        'b_ada': nrm(ks[5], (DEPTH, 6 * D_MODEL), 0.02),
        'norm_pre_mix': gain(ks[6], (DEPTH, D_MODEL)),
        'norm_post_mix': gain(ks[7], (DEPTH, D_MODEL)),
        'w_in': nrm(ks[8], (DEPTH, D_MODEL, D_IN), s),
        'sc_conv_w': nrm(ks[9], (DEPTH, SC_CONV, D_SC), SC_CONV ** -0.5),
        'ssm_conv_w': nrm(ks[10], (DEPTH, SSM_CONV, D_XBC), SSM_CONV ** -0.5),
        'ssm_conv_b': nrm(ks[11], (DEPTH, D_XBC), 0.02),
        'dt_bias_f': dt_f + jnp.log(-jnp.expm1(-dt_f)),
        'dt_bias_b': dt_b + jnp.log(-jnp.expm1(-dt_b)),
        'a_log_f': jnp.log(jax.random.uniform(ks[14], (DEPTH, SSM_HEADS), minval=1.0, maxval=16.0)),
        'a_log_b': jnp.log(jax.random.uniform(ks[15], (DEPTH, SSM_HEADS), minval=1.0, maxval=16.0)),
        'd_skip': 1.0 + 0.1 * jax.random.normal(ks[16], (DEPTH, SSM_HEADS), dtype=f32),
        'ssm_norm_w': gain(ks[17], (DEPTH, D_SSM)),
        'w_out': nrm(ks[18], (DEPTH, D_MIX, D_MODEL), D_MIX ** -0.5),
        'norm_pre_ffn': gain(ks[19], (DEPTH, D_MODEL)),
        'norm_post_ffn': gain(ks[20], (DEPTH, D_MODEL)),
        'w_gate': nrm(ks[21], (DEPTH, D_MODEL, D_FF), s),
        'w_up': nrm(ks[22], (DEPTH, D_MODEL, D_FF), s),
        'w_down': nrm(ks[23], (DEPTH, D_FF, D_MODEL), D_FF ** -0.5),
    }


def reference(x_prompt, x_sample, c_prompt, c_sample, w_ada, b_ada, norm_pre_mix,
              norm_post_mix, w_in, sc_conv_w, ssm_conv_w, ssm_conv_b, dt_bias_f, dt_bias_b,
              a_log_f, a_log_b, d_skip, ssm_norm_w, w_out, norm_pre_ffn, norm_post_ffn,
              w_gate, w_up, w_down):
    y_prompt = x_prompt
    y_sample = x_sample
    for i in range(DEPTH):
        layer = (w_ada[i], b_ada[i], norm_pre_mix[i], norm_post_mix[i], w_in[i], sc_conv_w[i],
                 ssm_conv_w[i], ssm_conv_b[i], dt_bias_f[i], dt_bias_b[i], a_log_f[i], a_log_b[i],
                 d_skip[i], ssm_norm_w[i], w_out[i], norm_pre_ffn[i], norm_post_ffn[i],
                 w_gate[i], w_up[i], w_down[i])
        y_prompt = encoder_layer(y_prompt, c_prompt, *layer)
        y_sample = encoder_layer(y_sample, c_sample, *layer)
    return (y_prompt, y_sample)
```

```python
import functools

import jax
import jax.numpy as jnp
from jax import lax
from jax.experimental import pallas as pl
from jax.experimental.pallas import tpu as pltpu

F32 = jnp.float32
BF16 = jnp.bfloat16

D_MODEL = 2048
D_SC = 1024
D_SSM = 1024
N_HEADS = 16
HEAD_DIM = 64
N_GROUPS = 4
D_STATE = 128
CHUNK = 128
D_XBC = 2048
D_MAIN = 3 * D_SC + D_SSM + D_XBC
D_DT = 2 * N_HEADS
D_FF = 5632
EPS = 1e-6
SC_CONV = 3
SSM_CONV = 5

LANES = 128
HALO_ROWS = 16
NEG_BIG = -1e30
VMEM_LIMIT = 56 * 1024 * 1024

TM_IN = 1024
TN_IN = 1024
R_SSD = 256
TM_OUT = 512
TM_FFN = 512
TF_FFN = 512
TN_MOD = 1024
MOD_ROWS = 16


def _params(sem):
    return pltpu.CompilerParams(dimension_semantics=sem, vmem_limit_bytes=VMEM_LIMIT)


def _silu(v):
    return v * jax.nn.sigmoid(v)


def _softplus(v):
    return jnp.maximum(v, 0.0) + jnp.log1p(jnp.exp(-jnp.abs(v)))


def _rms(v, w):
    return v * lax.rsqrt(jnp.mean(v * v, axis=-1, keepdims=True) + EPS) * w


def _mod_body(c_ref, w_ref, b_ref, o_ref):
    a = _silu(c_ref[...]).astype(BF16)
    o_ref[...] = jnp.dot(a, w_ref[...].astype(BF16), preferred_element_type=F32) + b_ref[...]


def _mod_call(c_all, w_ada, b_ada):
    n = w_ada.shape[1]
    return pl.pallas_call(
        _mod_body,
        grid=(n // TN_MOD,),
        in_specs=[
            pl.BlockSpec((MOD_ROWS, D_MODEL), lambda j: (0, 0)),
            pl.BlockSpec((D_MODEL, TN_MOD), lambda j: (0, j)),
            pl.BlockSpec((1, TN_MOD), lambda j: (0, j)),
        ],
        out_specs=pl.BlockSpec((MOD_ROWS, TN_MOD), lambda j: (0, j)),
        out_shape=jax.ShapeDtypeStruct((MOD_ROWS, n), F32),
        compiler_params=_params(("arbitrary",)),
        name="mod",
    )(c_all, w_ada, b_ada)


def _in_proj_body(x_ref, nw_ref, sh_ref, sc_ref, w_ref, wdt_ref, proj_ref, dt_ref, h_scr):
    @pl.when(pl.program_id(1) == 0)
    def _():
        h = _rms(x_ref[...], nw_ref[...]) * (1.0 + sc_ref[...]) + sh_ref[...]
        hb = h.astype(BF16)
        h_scr[...] = hb
        dt_ref[...] = jnp.dot(hb, wdt_ref[...], preferred_element_type=F32)

    proj_ref[...] = jnp.dot(h_scr[...], w_ref[...], preferred_element_type=F32).astype(BF16)


def _in_proj_call(x, mod3, seq_of, nw, w_main, w_dt):
    t = x.shape[0]
    return pl.pallas_call(
        _in_proj_body,
        grid=(t // TM_IN, D_MAIN // TN_IN),
        in_specs=[
            pl.BlockSpec((TM_IN, D_MODEL), lambda i, j: (i, 0)),
            pl.BlockSpec((1, D_MODEL), lambda i, j: (0, 0)),
            pl.BlockSpec((None, 1, D_MODEL), lambda i, j: (seq_of(i, TM_IN) * 6 + 0, 0, 0)),
            pl.BlockSpec((None, 1, D_MODEL), lambda i, j: (seq_of(i, TM_IN) * 6 + 1, 0, 0)),
            pl.BlockSpec((D_MODEL, TN_IN), lambda i, j: (0, j)),
            pl.BlockSpec((D_MODEL, LANES), lambda i, j: (0, 0)),
        ],
        out_specs=[
            pl.BlockSpec((TM_IN, TN_IN), lambda i, j: (i, j)),
            pl.BlockSpec((TM_IN, LANES), lambda i, j: (i, 0)),
        ],
        out_shape=[
            jax.ShapeDtypeStruct((t, D_MAIN), BF16),
            jax.ShapeDtypeStruct((t, LANES), F32),
        ],
        scratch_shapes=[pltpu.VMEM((TM_IN, D_MODEL), BF16)],
        compiler_params=_params(("arbitrary", "arbitrary")),
        name="in_proj",
    )(x, nw, mod3, mod3, w_main, w_dt)


def _dwconv(ext, w_ref, rows, k):
    p = (k - 1) // 2
    n = rows + HALO_ROWS
    out = None
    for i in range(k):
        d = p - i
        sh = ext if d == 0 else pltpu.roll(ext, d % n, axis=0)
        term = sh[8:rows + 8] * w_ref[i:i + 1, :]
        out = term if out is None else out + term
    return out


def _ext_rows(prev_ref, main_ref, next_ref, first, last):
    prev8 = jnp.where(first, 0.0, prev_ref[...].astype(F32)[8:16, :])
    next8 = jnp.where(last, 0.0, next_ref[...].astype(F32)[0:8, :])
    return jnp.concatenate([prev8, main_ref[...].astype(F32), next8], axis=0)


def _pair_rhs(xslab, lane):
    lo = jnp.where(lane < HEAD_DIM, xslab, 0.0)
    hi = xslab - lo
    return jnp.concatenate([lo.astype(BF16), hi.astype(BF16)], axis=0)


def _state_update(st_ref, g, half, bt_g, w_a, w_b, rhs, dec_a, dec_b, lane):
    lhs = jnp.concatenate([(bt_g * w_a).astype(BF16), (bt_g * w_b).astype(BF16)], axis=1)
    ds = jnp.dot(lhs, rhs, preferred_element_type=F32)
    dec = jnp.where(lane[0:1, :] < HEAD_DIM, dec_a, dec_b)
    sl = slice(half * LANES, (half + 1) * LANES)
    st_ref[g, :, sl] = st_ref[g, :, sl] * dec + ds


def _ssd_fwd_body(bps, xbc_ref, prev_ref, next_ref, dt_ref, cw_ref, cb_ref, bias_ref, a_ref,
                  trit_ref, u_ref, yo_ref, st_ref):
    i = pl.program_id(0)
    first = (i % bps) == 0
    last = (i % bps) == bps - 1
    rows = R_SSD

    @pl.when(first)
    def _():
        st_ref[...] = jnp.zeros_like(st_ref)

    ext = _ext_rows(prev_ref, xbc_ref, next_ref, first, last)
    u = _silu(_dwconv(ext, cw_ref, rows, SSM_CONV) + cb_ref[...])
    u_ref[...] = u.astype(BF16)

    raw_t = jnp.transpose(dt_ref[...])[0:N_HEADS, :]
    lane = lax.broadcasted_iota(jnp.int32, (CHUNK, LANES), 1)
    trit = trit_ref[...]

    for c in range(rows // CHUNK):
        r0 = c * CHUNK
        dt_c = _softplus(raw_t[:, r0:r0 + CHUNK] + bias_ref[0:N_HEADS, :])
        adt_c = dt_c * a_ref[0:N_HEADS, :]
        a_c = jnp.dot(adt_c, trit, precision=lax.Precision.HIGHEST, preferred_element_type=F32)
        a_end = jnp.broadcast_to(a_c[:, CHUNK - 1:CHUNK], a_c.shape)
        w_c = jnp.exp(a_end - a_c) * dt_c
        dec_c = jnp.exp(a_end)

        xs_c = u[r0:r0 + CHUNK, 0:D_SSM]
        b_c = u[r0:r0 + CHUNK, D_SSM:D_SSM + N_GROUPS * D_STATE]
        c_c = u[r0:r0 + CHUNK, D_SSM + N_GROUPS * D_STATE:]
        for g in range(N_GROUPS):
            gs = slice(g * D_STATE, (g + 1) * D_STATE)
            yo_g = jnp.dot(c_c[:, gs].astype(BF16), st_ref[g].astype(BF16),
                           preferred_element_type=F32)
            yo_ref[r0:r0 + CHUNK, g * 256:(g + 1) * 256] = yo_g.astype(BF16)
            bt_g = jnp.transpose(b_c[:, gs])
            for half in range(2):
                k = 2 * g + half
                ha, hb = 2 * k, 2 * k + 1
                rhs = _pair_rhs(xs_c[:, k * LANES:(k + 1) * LANES], lane)
                _state_update(st_ref, g, half, bt_g, w_c[ha:ha + 1, :], w_c[hb:hb + 1, :], rhs,
                              dec_c[ha:ha + 1, :], dec_c[hb:hb + 1, :], lane)


def _ssd_fwd_call(proj, dt_raw, seq_len, cw, cb, bias_rep, a_rep, trit):
    t = proj.shape[0]
    nb = t // R_SSD
    bps = seq_len // R_SSD
    hb = R_SSD // HALO_ROWS
    nhalo = t // HALO_ROWS
    xcol = (3 * D_SC + D_SSM) // D_XBC
    return pl.pallas_call(
        functools.partial(_ssd_fwd_body, bps),
        grid=(nb,),
        in_specs=[
            pl.BlockSpec((R_SSD, D_XBC), lambda i: (i, xcol)),
            pl.BlockSpec((HALO_ROWS, D_XBC), lambda i: (jnp.maximum(i * hb - 1, 0), xcol)),
            pl.BlockSpec((HALO_ROWS, D_XBC), lambda i: (jnp.minimum((i + 1) * hb, nhalo - 1), xcol)),
            pl.BlockSpec((R_SSD, LANES), lambda i: (i, 0)),
            pl.BlockSpec((SSM_CONV, D_XBC), lambda i: (0, 0)),
            pl.BlockSpec((1, D_XBC), lambda i: (0, 0)),
            pl.BlockSpec((D_DT, LANES), lambda i: (0, 0)),
            pl.BlockSpec((D_DT, LANES), lambda i: (0, 0)),
            pl.BlockSpec((CHUNK, CHUNK), lambda i: (0, 0)),
        ],
        out_specs=[
            pl.BlockSpec((R_SSD, D_XBC), lambda i: (i, 0)),
            pl.BlockSpec((R_SSD, D_SSM), lambda i: (i, 0)),
        ],
        out_shape=[
            jax.ShapeDtypeStruct((t, D_XBC), BF16),
            jax.ShapeDtypeStruct((t, D_SSM), BF16),
        ],
        scratch_shapes=[pltpu.VMEM((N_GROUPS, D_STATE, 4 * HEAD_DIM), F32)],
        compiler_params=_params(("arbitrary",)),
        name="ssd_fwd",
    )(proj, proj, proj, dt_raw, cw, cb, bias_rep, a_rep, trit)


def _mix_body(bps, nb, scb_ref, scc_ref, scx_ref, z_ref, cprev_ref, cnext_ref, xprev_ref, xnext_ref,
              u_ref, yo_ref, dt_ref, scw_ref, bias_ref, a_ref, biasr_ref, ar_ref, tri_ref, trit_ref,
              dskip_ref, nw_ref, o_ref, st_ref, y_scr):
    bi = nb - 1 - pl.program_id(0)
    first = (bi % bps) == 0
    last = (bi % bps) == bps - 1
    rows = R_SSD

    @pl.when(last)
    def _():
        st_ref[...] = jnp.zeros_like(st_ref)

    v_ext = (_ext_rows(cprev_ref, scc_ref, cnext_ref, first, last)
             * _ext_rows(xprev_ref, scx_ref, xnext_ref, first, last))
    y_a = scb_ref[...].astype(F32) * _dwconv(v_ext, scw_ref, rows, SC_CONV)
    o_ref[:, 0:D_SC] = y_a.astype(BF16)

    raw = dt_ref[...]
    raw_t = jnp.transpose(raw)[0:D_DT, :]
    lane = lax.broadcasted_iota(jnp.int32, (CHUNK, LANES), 1)
    row = lax.broadcasted_iota(jnp.int32, (CHUNK, LANES), 0)
    lmask = row >= lane
    umask = lane >= row
    tri = tri_ref[...]
    trit = trit_ref[...]

    for c in reversed(range(rows // CHUNK)):
        r0 = c * CHUNK
        dt_r = _softplus(raw_t[:, r0:r0 + CHUNK] + bias_ref[...])
        adt_r = dt_r * a_ref[...]
        a_r = jnp.dot(adt_r, trit, precision=lax.Precision.HIGHEST, preferred_element_type=F32)
        e_r = a_r - adt_r
        a_end = jnp.broadcast_to(a_r[:, CHUNK - 1:CHUNK], a_r.shape)
        wb_r = jnp.exp(e_r) * dt_r
        dec_r = jnp.exp(a_end)
        dt_cf = _softplus(raw[r0:r0 + CHUNK, :] + biasr_ref[...])
        adt_cf = dt_cf * ar_ref[...]
        a_cf = jnp.dot(tri, adt_cf, precision=lax.Precision.HIGHEST, preferred_element_type=F32)
        e_cf = a_cf - adt_cf

        u_c = u_ref[r0:r0 + CHUNK, :]
        xs_c = u_c[:, 0:D_SSM].astype(F32)
        b_c = u_c[:, D_SSM:D_SSM + N_GROUPS * D_STATE]
        c_c = u_c[:, D_SSM + N_GROUPS * D_STATE:]
        slabs = []
        for g in range(N_GROUPS):
            gs = slice(g * D_STATE, (g + 1) * D_STATE)
            cb_g = lax.dot_general(c_c[:, gs], b_c[:, gs], (((1,), (1,)), ((), ())),
                                   preferred_element_type=F32)
            yob_g = jnp.dot(c_c[:, gs], st_ref[g].astype(BF16), preferred_element_type=F32)
            bt_g = jnp.transpose(b_c[:, gs].astype(F32))
            for half in range(2):
                k = 2 * g + half
                ks = slice(k * LANES, (k + 1) * LANES)
                xslab = xs_c[:, ks]
                rhs = _pair_rhs(xslab, lane)
                ms, colf, colb = [], [], []
                for h in (2 * k, 2 * k + 1):
                    hb_ = N_HEADS + h
                    cf = jnp.broadcast_to(a_cf[:, h:h + 1], (CHUNK, LANES))
                    cbk = jnp.broadcast_to(e_cf[:, hb_:hb_ + 1], (CHUNK, LANES))
                    lf = jnp.exp(jnp.where(lmask, cf - a_r[h:h + 1, :], NEG_BIG))
                    lb = jnp.exp(jnp.where(umask, e_r[hb_:hb_ + 1, :] - cbk, NEG_BIG))
                    w = lf * dt_r[h:h + 1, :] + lb * dt_r[hb_:hb_ + 1, :]
                    ms.append((cb_g * w).astype(BF16))
                    colf.append(cf)
                    colb.append(a_end[hb_:hb_ + 1, :] - cbk)
                y_diag = jnp.dot(jnp.concatenate(ms, axis=1), rhs, preferred_element_type=F32)
                sel = lane < HEAD_DIM
                rs_f = jnp.exp(jnp.where(sel, colf[0], colf[1]))
                rs_b = jnp.exp(jnp.where(sel, colb[0], colb[1]))
                hs = slice(half * LANES, (half + 1) * LANES)
                y_slab = (y_diag + rs_f * yo_ref[r0:r0 + CHUNK, ks].astype(F32)
                          + rs_b * yob_g[:, hs] + xslab * dskip_ref[:, ks])
                slabs.append(y_slab)
                ha, hb2 = N_HEADS + 2 * k, N_HEADS + 2 * k + 1
                _state_update(st_ref, g, half, bt_g, wb_r[ha:ha + 1, :], wb_r[hb2:hb2 + 1, :], rhs,
                              dec_r[ha:ha + 1, :], dec_r[hb2:hb2 + 1, :], lane)
        y_scr[r0:r0 + CHUNK, :] = jnp.concatenate(slabs, axis=1)

    y = y_scr[...] * _silu(z_ref[...].astype(F32))
    gw = D_SSM // N_GROUPS
    for g in range(N_GROUPS):
        gs = slice(g * gw, (g + 1) * gw)
        yg = y[:, gs]
        yn = yg * lax.rsqrt(jnp.mean(yg * yg, axis=-1, keepdims=True) + EPS) * nw_ref[:, gs]
        o_ref[:, D_SC + g * gw:D_SC + (g + 1) * gw] = yn.astype(BF16)


def _mix_call(proj, u, yo_f, dt_raw, seq_len, scw, bias_rep, a_rep, bias_row, a_row, tri, trit,
              dskip, nw):
    t = proj.shape[0]
    nb = t // R_SSD
    bps = seq_len // R_SSD
    hb = R_SSD // HALO_ROWS
    nhalo = t // HALO_ROWS

    def blk(i):
        return nb - 1 - i

    def prev_blk(i):
        return jnp.maximum(blk(i) * hb - 1, 0)

    def next_blk(i):
        return jnp.minimum((blk(i) + 1) * hb, nhalo - 1)

    const = lambda i: (0, 0)
    return pl.pallas_call(
        functools.partial(_mix_body, bps, nb),
        grid=(nb,),
        in_specs=[
            pl.BlockSpec((R_SSD, D_SC), lambda i: (blk(i), 0)),
            pl.BlockSpec((R_SSD, D_SC), lambda i: (blk(i), 1)),
            pl.BlockSpec((R_SSD, D_SC), lambda i: (blk(i), 2)),
            pl.BlockSpec((R_SSD, D_SSM), lambda i: (blk(i), 3)),
            pl.BlockSpec((HALO_ROWS, D_SC), lambda i: (prev_blk(i), 1)),
            pl.BlockSpec((HALO_ROWS, D_SC), lambda i: (next_blk(i), 1)),
            pl.BlockSpec((HALO_ROWS, D_SC), lambda i: (prev_blk(i), 2)),
            pl.BlockSpec((HALO_ROWS, D_SC), lambda i: (next_blk(i), 2)),
            pl.BlockSpec((R_SSD, D_XBC), lambda i: (blk(i), 0)),
            pl.BlockSpec((R_SSD, D_SSM), lambda i: (blk(i), 0)),
            pl.BlockSpec((R_SSD, LANES), lambda i: (blk(i), 0)),
            pl.BlockSpec((SC_CONV, D_SC), const),
            pl.BlockSpec((D_DT, LANES), const),
            pl.BlockSpec((D_DT, LANES), const),
            pl.BlockSpec((1, LANES), const),
            pl.BlockSpec((1, LANES), const),
            pl.BlockSpec((CHUNK, CHUNK), const),
            pl.BlockSpec((CHUNK, CHUNK), const),
            pl.BlockSpec((1, D_SSM), const),
            pl.BlockSpec((1, D_SSM), const),
        ],
        out_specs=pl.BlockSpec((R_SSD, D_MODEL), lambda i: (blk(i), 0)),
        out_shape=jax.ShapeDtypeStruct((t, D_MODEL), BF16),
        scratch_shapes=[
            pltpu.VMEM((N_GROUPS, D_STATE, 4 * HEAD_DIM), F32),
            pltpu.VMEM((R_SSD, D_SSM), F32),
        ],
        compiler_params=_params(("arbitrary",)),
        name="mix",
    )(proj, proj, proj, proj, proj, proj, proj, proj, u, yo_f, dt_raw, scw, bias_rep, a_rep,
      bias_row, a_row, tri, trit, dskip, nw)


def _out_proj_body(y_ref, w_ref, x_ref, nw_ref, g_ref, o_ref):
    m = jnp.dot(y_ref[...], w_ref[...], preferred_element_type=F32)
    o_ref[...] = x_ref[...] + g_ref[...] * _rms(m, nw_ref[...])


def _out_proj_call(y_mix, x, mod3, seq_of, w_out, nw):
    t = x.shape[0]
    return pl.pallas_call(
        _out_proj_body,
        grid=(t // TM_OUT,),
        in_specs=[
            pl.BlockSpec((TM_OUT, D_MODEL), lambda i: (i, 0)),
            pl.BlockSpec((D_MODEL, D_MODEL), lambda i: (0, 0)),
            pl.BlockSpec((TM_OUT, D_MODEL), lambda i: (i, 0)),
            pl.BlockSpec((1, D_MODEL), lambda i: (0, 0)),
            pl.BlockSpec((None, 1, D_MODEL), lambda i: (seq_of(i, TM_OUT) * 6 + 2, 0, 0)),
        ],
        out_specs=pl.BlockSpec((TM_OUT, D_MODEL), lambda i: (i, 0)),
        out_shape=jax.ShapeDtypeStruct((t, D_MODEL), F32),
        compiler_params=_params(("arbitrary",)),
        name="out_proj",
    )(y_mix, w_out, x, nw, mod3)


def _ffn_body(x_ref, nw1_ref, sh_ref, sc_ref, wg_ref, wu_ref, wd_ref, nw2_ref, g_ref, o_ref, h_scr):
    j = pl.program_id(1)

    @pl.when(j == 0)
    def _():
        h = _rms(x_ref[...], nw1_ref[...]) * (1.0 + sc_ref[...]) + sh_ref[...]
        h_scr[...] = h.astype(BF16)

    hb = h_scr[...]
    gate = jnp.dot(hb, wg_ref[...], preferred_element_type=F32)
    up = jnp.dot(hb, wu_ref[...], preferred_element_type=F32)
    act = (_silu(gate) * up).astype(BF16)
    part = jnp.dot(act, wd_ref[...], preferred_element_type=F32)

    @pl.when(j == 0)
    def _():
        o_ref[...] = part

    @pl.when(j > 0)
    def _():
        o_ref[...] += part

    @pl.when(j == pl.num_programs(1) - 1)
    def _():
        o_ref[...] = x_ref[...] + g_ref[...] * _rms(o_ref[...], nw2_ref[...])


def _ffn_call(x1, mod3, seq_of, nw1, wg, wu, wd, nw2):
    t = x1.shape[0]
    return pl.pallas_call(
        _ffn_body,
        grid=(t // TM_FFN, D_FF // TF_FFN),
        in_specs=[
            pl.BlockSpec((TM_FFN, D_MODEL), lambda i, j: (i, 0)),
            pl.BlockSpec((1, D_MODEL), lambda i, j: (0, 0)),
            pl.BlockSpec((None, 1, D_MODEL), lambda i, j: (seq_of(i, TM_FFN) * 6 + 3, 0, 0)),
            pl.BlockSpec((None, 1, D_MODEL), lambda i, j: (seq_of(i, TM_FFN) * 6 + 4, 0, 0)),
            pl.BlockSpec((D_MODEL, TF_FFN), lambda i, j: (0, j)),
            pl.BlockSpec((D_MODEL, TF_FFN), lambda i, j: (0, j)),
            pl.BlockSpec((TF_FFN, D_MODEL), lambda i, j: (j, 0)),
            pl.BlockSpec((1, D_MODEL), lambda i, j: (0, 0)),
            pl.BlockSpec((None, 1, D_MODEL), lambda i, j: (seq_of(i, TM_FFN) * 6 + 5, 0, 0)),
        ],
        out_specs=pl.BlockSpec((TM_FFN, D_MODEL), lambda i, j: (i, 0)),
        out_shape=jax.ShapeDtypeStruct((t, D_MODEL), F32),
        scratch_shapes=[pltpu.VMEM((TM_FFN, D_MODEL), BF16)],
        compiler_params=_params(("arbitrary", "arbitrary")),
        name="ffn",
    )(x1, nw1, mod3, mod3, wg, wu, wd, nw2, mod3)


def _layer(x, mod3, seq_base, seq_len, p):
    def seq_of(i, tm):
        return seq_base + (i * tm) // seq_len

    proj, dt_raw = _in_proj_call(x, mod3, seq_of, p["norm_pre_mix"], p["w_in_main"], p["w_in_dt"])
    u, yo_f = _ssd_fwd_call(proj, dt_raw, seq_len, p["ssm_conv_w"], p["ssm_conv_b"],
                            p["bias_rep"], p["a_rep"], p["trit"])
    y_mix = _mix_call(proj, u, yo_f, dt_raw, seq_len, p["sc_conv_w"], p["bias_rep"], p["a_rep"],
                      p["bias_row"], p["a_row"], p["tri"], p["trit"], p["dskip"], p["ssm_norm_w"])
    x1 = _out_proj_call(y_mix, x, mod3, seq_of, p["w_out"], p["norm_post_mix"])
    return _ffn_call(x1, mod3, seq_of, p["norm_pre_ffn"], p["w_gate"], p["w_up"], p["w_down"],
                     p["norm_post_ffn"])


def _prep_layer(i, w_ada, b_ada, norm_pre_mix, norm_post_mix, w_in, sc_conv_w, ssm_conv_w,
                ssm_conv_b, dt_bias_f, dt_bias_b, a_log_f, a_log_b, d_skip, ssm_norm_w, w_out,
                norm_pre_ffn, norm_post_ffn, w_gate, w_up, w_down):
    row = lambda v: v[i].astype(F32).reshape(1, -1)
    w_in_i = w_in[i]
    w_dt = jnp.pad(w_in_i[:, D_MAIN:], ((0, 0), (0, LANES - D_DT))).astype(BF16)
    dt_bias = jnp.concatenate([dt_bias_f[i], dt_bias_b[i]]).astype(F32)
    a = -jnp.exp(jnp.concatenate([a_log_f[i], a_log_b[i]]).astype(F32))
    pad_row = lambda v: jnp.pad(v, (0, LANES - D_DT)).reshape(1, LANES)
    tri = jnp.tril(jnp.ones((CHUNK, CHUNK), F32))
    return {
        "w_ada": w_ada[i], "b_ada": row(b_ada),
        "norm_pre_mix": row(norm_pre_mix), "norm_post_mix": row(norm_post_mix),
        "w_in_main": w_in_i[:, :D_MAIN].astype(BF16), "w_in_dt": w_dt,
        "sc_conv_w": sc_conv_w[i].astype(F32), "ssm_conv_w": ssm_conv_w[i].astype(F32),
        "ssm_conv_b": row(ssm_conv_b),
        "bias_rep": jnp.broadcast_to(dt_bias[:, None], (D_DT, LANES)),
        "a_rep": jnp.broadcast_to(a[:, None], (D_DT, LANES)),
        "bias_row": pad_row(dt_bias), "a_row": pad_row(a),
        "tri": tri, "trit": tri.T,
        "dskip": jnp.repeat(d_skip[i].astype(F32), HEAD_DIM).reshape(1, D_SSM),
        "ssm_norm_w": row(ssm_norm_w),
        "w_out": w_out[i].astype(BF16),
        "norm_pre_ffn": row(norm_pre_ffn), "norm_post_ffn": row(norm_post_ffn),
        "w_gate": w_gate[i].astype(BF16), "w_up": w_up[i].astype(BF16),
        "w_down": w_down[i].astype(BF16),
    }


def kernel(x_prompt, x_sample, c_prompt, c_sample, w_ada, b_ada, norm_pre_mix, norm_post_mix, w_in,
           sc_conv_w, ssm_conv_w, ssm_conv_b, dt_bias_f, dt_bias_b, a_log_f, a_log_b, d_skip,
           ssm_norm_w, w_out, norm_pre_ffn, norm_post_ffn, w_gate, w_up, w_down):
    bp, lp, d = x_prompt.shape
    bs, ls, _ = x_sample.shape
    assert d == D_MODEL and bp + bs <= MOD_ROWS
    for n in (lp, ls):
        assert n % max(TM_IN, R_SSD, TM_OUT, TM_FFN) == 0
    yp = x_prompt.reshape(bp * lp, d)
    ys = x_sample.reshape(bs * ls, d)
    c_all = jnp.concatenate(
        [c_prompt, c_sample, jnp.zeros((MOD_ROWS - bp - bs, d), c_prompt.dtype)], axis=0).astype(F32)
    for i in range(w_ada.shape[0]):
        p = _prep_layer(i, w_ada, b_ada, norm_pre_mix, norm_post_mix, w_in, sc_conv_w, ssm_conv_w,
                        ssm_conv_b, dt_bias_f, dt_bias_b, a_log_f, a_log_b, d_skip, ssm_norm_w,
                        w_out, norm_pre_ffn, norm_post_ffn, w_gate, w_up, w_down)
        mod = _mod_call(c_all, p["w_ada"], p["b_ada"])
        mod3 = mod.reshape(MOD_ROWS * 6, 1, D_MODEL)
        yp = _layer(yp, mod3, 0, lp, p)
        ys = _layer(ys, mod3, bp, ls, p)
    return yp.reshape(x_prompt.shape), ys.reshape(x_sample.shape)
```

```python
import functools

import jax
import jax.numpy as jnp
from jax import lax
from jax.experimental import pallas as pl
from jax.experimental.pallas import tpu as pltpu

F32 = jnp.float32
BF16 = jnp.bfloat16

D_MODEL = 2048
D_SC = 1024
D_SSM = 1024
N_HEADS = 16
HEAD_DIM = 64
N_GROUPS = 4
D_STATE = 128
CHUNK = 128
D_XBC = 2048
D_MAIN = 3 * D_SC + D_SSM + D_XBC
D_DT = 2 * N_HEADS
D_FF = 5632
EPS = 1e-6
SC_CONV = 3
SSM_CONV = 5

LANES = 128
HALO_ROWS = 16
NEG_BIG = -1e30
VMEM_LIMIT = 56 * 1024 * 1024

TM_IN = 1024
TN_IN = 1024
R_SSD = 256
TM_OUT = 512
IN_SUB = 256
OUT_SUB = 256
TM_FFN = 512
TF_FFN = 512
FFN_SUB = 256
TN_MOD = 1024
MOD_ROWS = 16


def _params(sem):
    return pltpu.CompilerParams(dimension_semantics=sem, vmem_limit_bytes=VMEM_LIMIT)


def _silu(v):
    return v * jax.nn.sigmoid(v)


def _softplus(v):
    return jnp.maximum(v, 0.0) + jnp.log1p(jnp.exp(-jnp.abs(v)))


def _rms(v, w):
    return v * lax.rsqrt(jnp.mean(v * v, axis=-1, keepdims=True) + EPS) * w


def _mod_body(c_ref, w_ref, b_ref, o_ref):
    a = _silu(c_ref[...]).astype(BF16)
    o_ref[...] = jnp.dot(a, w_ref[...].astype(BF16), preferred_element_type=F32) + b_ref[...]


def _mod_call(c_all, w_ada, b_ada):
    n = w_ada.shape[1]
    return pl.pallas_call(
        _mod_body,
        grid=(n // TN_MOD,),
        in_specs=[
            pl.BlockSpec((MOD_ROWS, D_MODEL), lambda j: (0, 0)),
            pl.BlockSpec((D_MODEL, TN_MOD), lambda j: (0, j)),
            pl.BlockSpec((1, TN_MOD), lambda j: (0, j)),
        ],
        out_specs=pl.BlockSpec((MOD_ROWS, TN_MOD), lambda j: (0, j)),
        out_shape=jax.ShapeDtypeStruct((MOD_ROWS, n), F32),
        compiler_params=_params(("arbitrary",)),
        name="mod",
    )(c_all, w_ada, b_ada)


def _in_proj_body(x_ref, nw_ref, sh_ref, sc_ref, w_ref, wdt_ref, bias_ref, proj_ref, dt_ref, h_scr):
    j = pl.program_id(1)

    @pl.when(j == 0)
    def _():
        for r in range(TM_IN // IN_SUB):
            rs = slice(r * IN_SUB, (r + 1) * IN_SUB)
            h = _rms(x_ref[rs, :], nw_ref[...]) * (1.0 + sc_ref[...]) + sh_ref[...]
            hb = h.astype(BF16)
            h_scr[rs, :] = hb
            dt_raw = jnp.dot(hb, wdt_ref[...], preferred_element_type=F32)
            dt_ref[rs, :] = _softplus(dt_raw + bias_ref[...])
            proj_ref[rs, :] = jnp.dot(hb, w_ref[...], preferred_element_type=F32).astype(BF16)

    @pl.when(j > 0)
    def _():
        proj_ref[...] = jnp.dot(h_scr[...], w_ref[...], preferred_element_type=F32).astype(BF16)


def _in_proj_call(x, mod3, seq_of, nw, w_main, w_dt, bias_row):
    t = x.shape[0]
    return pl.pallas_call(
        _in_proj_body,
        grid=(t // TM_IN, D_MAIN // TN_IN),
        in_specs=[
            pl.BlockSpec((TM_IN, D_MODEL), lambda i, j: (i, 0)),
            pl.BlockSpec((1, D_MODEL), lambda i, j: (0, 0)),
            pl.BlockSpec((None, 1, D_MODEL), lambda i, j: (seq_of(i, TM_IN) * 6 + 0, 0, 0)),
            pl.BlockSpec((None, 1, D_MODEL), lambda i, j: (seq_of(i, TM_IN) * 6 + 1, 0, 0)),
            pl.BlockSpec((D_MODEL, TN_IN), lambda i, j: (0, j)),
            pl.BlockSpec((D_MODEL, LANES), lambda i, j: (0, 0)),
            pl.BlockSpec((1, LANES), lambda i, j: (0, 0)),
        ],
        out_specs=[
            pl.BlockSpec((TM_IN, TN_IN), lambda i, j: (i, j)),
            pl.BlockSpec((TM_IN, LANES), lambda i, j: (i, 0)),
        ],
        out_shape=[
            jax.ShapeDtypeStruct((t, D_MAIN), BF16),
            jax.ShapeDtypeStruct((t, LANES), F32),
        ],
        scratch_shapes=[pltpu.VMEM((TM_IN, D_MODEL), BF16)],
        compiler_params=_params(("arbitrary", "arbitrary")),
        name="in_proj",
    )(x, nw, mod3, mod3, w_main, w_dt, bias_row)


def _dwconv(ext, w_ref, rows, k):
    p = (k - 1) // 2
    n = rows + HALO_ROWS
    out = None
    for i in range(k):
        d = p - i
        sh = ext if d == 0 else pltpu.roll(ext, d % n, axis=0)
        term = sh[8:rows + 8] * w_ref[i:i + 1, :]
        out = term if out is None else out + term
    return out


def _ext_rows(prev_ref, main_ref, next_ref, first, last):
    prev8 = jnp.where(first, 0.0, prev_ref[...].astype(F32)[8:16, :])
    next8 = jnp.where(last, 0.0, next_ref[...].astype(F32)[0:8, :])
    return jnp.concatenate([prev8, main_ref[...].astype(F32), next8], axis=0)


def _pair_rhs(xslab, lane):
    lo = jnp.where(lane < HEAD_DIM, xslab, 0.0)
    hi = xslab - lo
    return jnp.concatenate([lo.astype(BF16), hi.astype(BF16)], axis=0)


def _state_update(st_ref, g, half, bt_g, w_a, w_b, rhs, dec_a, dec_b, lane):
    lhs = jnp.concatenate([(bt_g * w_a).astype(BF16), (bt_g * w_b).astype(BF16)], axis=1)
    ds = jnp.dot(lhs, rhs, preferred_element_type=F32)
    dec = jnp.where(lane[0:1, :] < HEAD_DIM, dec_a, dec_b)
    sl = slice(half * LANES, (half + 1) * LANES)
    st_ref[g, :, sl] = st_ref[g, :, sl] * dec + ds


def _ssd_fwd_body(bps, xbc_ref, prev_ref, next_ref, dt_ref, cw_ref, cb_ref, a_ref,
                  trit_ref, u_ref, yo_ref, st_ref):
    i = pl.program_id(0)
    first = (i % bps) == 0
    last = (i % bps) == bps - 1
    rows = R_SSD

    @pl.when(first)
    def _():
        st_ref[...] = jnp.zeros_like(st_ref)

    ext = _ext_rows(prev_ref, xbc_ref, next_ref, first, last)
    u = _silu(_dwconv(ext, cw_ref, rows, SSM_CONV) + cb_ref[...])
    u_ref[...] = u.astype(BF16)

    dt_t = jnp.transpose(dt_ref[...])[0:N_HEADS, :]
    lane = lax.broadcasted_iota(jnp.int32, (CHUNK, LANES), 1)
    trit = trit_ref[...]

    for c in range(rows // CHUNK):
        r0 = c * CHUNK
        dt_c = dt_t[:, r0:r0 + CHUNK]
        adt_c = dt_c * a_ref[0:N_HEADS, :]
        a_c = jnp.dot(adt_c, trit, precision=lax.Precision.HIGHEST, preferred_element_type=F32)
        a_end = jnp.broadcast_to(a_c[:, CHUNK - 1:CHUNK], a_c.shape)
        w_c = jnp.exp(a_end - a_c) * dt_c
        dec_c = jnp.exp(a_end)

        xs_c = u[r0:r0 + CHUNK, 0:D_SSM]
        b_c = u[r0:r0 + CHUNK, D_SSM:D_SSM + N_GROUPS * D_STATE]
        c_c = u[r0:r0 + CHUNK, D_SSM + N_GROUPS * D_STATE:]
        for g in range(N_GROUPS):
            gs = slice(g * D_STATE, (g + 1) * D_STATE)
            yo_g = jnp.dot(c_c[:, gs].astype(BF16), st_ref[g].astype(BF16),
                           preferred_element_type=F32)
            yo_ref[r0:r0 + CHUNK, g * 256:(g + 1) * 256] = yo_g.astype(BF16)
            bt_g = jnp.transpose(b_c[:, gs])
            for half in range(2):
                k = 2 * g + half
                ha, hb = 2 * k, 2 * k + 1
                rhs = _pair_rhs(xs_c[:, k * LANES:(k + 1) * LANES], lane)
                _state_update(st_ref, g, half, bt_g, w_c[ha:ha + 1, :], w_c[hb:hb + 1, :], rhs,
                              dec_c[ha:ha + 1, :], dec_c[hb:hb + 1, :], lane)


def _ssd_fwd_call(proj, dt, seq_len, cw, cb, a_rep, trit):
    t = proj.shape[0]
    nb = t // R_SSD
    bps = seq_len // R_SSD
    hb = R_SSD // HALO_ROWS
    nhalo = t // HALO_ROWS
    xcol = (3 * D_SC + D_SSM) // D_XBC
    return pl.pallas_call(
        functools.partial(_ssd_fwd_body, bps),
        grid=(nb,),
        in_specs=[
            pl.BlockSpec((R_SSD, D_XBC), lambda i: (i, xcol)),
            pl.BlockSpec((HALO_ROWS, D_XBC), lambda i: (jnp.maximum(i * hb - 1, 0), xcol)),
            pl.BlockSpec((HALO_ROWS, D_XBC), lambda i: (jnp.minimum((i + 1) * hb, nhalo - 1), xcol)),
            pl.BlockSpec((R_SSD, LANES), lambda i: (i, 0)),
            pl.BlockSpec((SSM_CONV, D_XBC), lambda i: (0, 0)),
            pl.BlockSpec((1, D_XBC), lambda i: (0, 0)),
            pl.BlockSpec((D_DT, LANES), lambda i: (0, 0)),
            pl.BlockSpec((CHUNK, CHUNK), lambda i: (0, 0)),
        ],
        out_specs=[
            pl.BlockSpec((R_SSD, D_XBC), lambda i: (i, 0)),
            pl.BlockSpec((R_SSD, D_SSM), lambda i: (i, 0)),
        ],
        out_shape=[
            jax.ShapeDtypeStruct((t, D_XBC), BF16),
            jax.ShapeDtypeStruct((t, D_SSM), BF16),
        ],
        scratch_shapes=[pltpu.VMEM((N_GROUPS, D_STATE, 4 * HEAD_DIM), F32)],
        compiler_params=_params(("arbitrary",)),
        name="ssd_fwd",
    )(proj, proj, proj, dt, cw, cb, a_rep, trit)


def _mix_body(bps, nb, scb_ref, scc_ref, scx_ref, z_ref, cprev_ref, cnext_ref, xprev_ref, xnext_ref,
              u_ref, yo_ref, dt_ref, scw_ref, a_ref, ar_ref, tri_ref, trit_ref,
              dskip_ref, nw_ref, o_ref, st_ref, y_scr):
    bi = nb - 1 - pl.program_id(0)
    first = (bi % bps) == 0
    last = (bi % bps) == bps - 1
    rows = R_SSD

    @pl.when(last)
    def _():
        st_ref[...] = jnp.zeros_like(st_ref)

    v_ext = (_ext_rows(cprev_ref, scc_ref, cnext_ref, first, last)
             * _ext_rows(xprev_ref, scx_ref, xnext_ref, first, last))
    y_a = scb_ref[...].astype(F32) * _dwconv(v_ext, scw_ref, rows, SC_CONV)
    o_ref[:, 0:D_SC] = y_a.astype(BF16)

    dt = dt_ref[...]
    dt_t = jnp.transpose(dt)[0:D_DT, :]
    lane = lax.broadcasted_iota(jnp.int32, (CHUNK, LANES), 1)
    row = lax.broadcasted_iota(jnp.int32, (CHUNK, LANES), 0)
    lmask = row >= lane
    umask = lane >= row
    tri = tri_ref[...]
    trit = trit_ref[...]

    for c in reversed(range(rows // CHUNK)):
        r0 = c * CHUNK
        dt_r = dt_t[:, r0:r0 + CHUNK]
        adt_r = dt_r * a_ref[...]
        a_r = jnp.dot(adt_r, trit, precision=lax.Precision.HIGHEST, preferred_element_type=F32)
        e_r = a_r - adt_r
        a_end = jnp.broadcast_to(a_r[:, CHUNK - 1:CHUNK], a_r.shape)
        wb_r = jnp.exp(e_r) * dt_r
        dec_r = jnp.exp(a_end)
        dt_cf = dt[r0:r0 + CHUNK, :]
        adt_cf = dt_cf * ar_ref[...]
        a_cf = jnp.dot(tri, adt_cf, precision=lax.Precision.HIGHEST, preferred_element_type=F32)
        e_cf = a_cf - adt_cf

        u_c = u_ref[r0:r0 + CHUNK, :]
        xs_c = u_c[:, 0:D_SSM].astype(F32)
        b_c = u_c[:, D_SSM:D_SSM + N_GROUPS * D_STATE]
        c_c = u_c[:, D_SSM + N_GROUPS * D_STATE:]
        slabs = []
        for g in range(N_GROUPS):
            gs = slice(g * D_STATE, (g + 1) * D_STATE)
            cb_g = lax.dot_general(c_c[:, gs], b_c[:, gs], (((1,), (1,)), ((), ())),
                                   preferred_element_type=F32)
            yob_g = jnp.dot(c_c[:, gs], st_ref[g].astype(BF16), preferred_element_type=F32)
            bt_g = jnp.transpose(b_c[:, gs].astype(F32))
            for half in range(2):
                k = 2 * g + half
                ks = slice(k * LANES, (k + 1) * LANES)
                xslab = xs_c[:, ks]
                rhs = _pair_rhs(xslab, lane)
                ms, colf, colb = [], [], []
                for h in (2 * k, 2 * k + 1):
                    hb_ = N_HEADS + h
                    cf = jnp.broadcast_to(a_cf[:, h:h + 1], (CHUNK, LANES))
                    cbk = jnp.broadcast_to(e_cf[:, hb_:hb_ + 1], (CHUNK, LANES))
                    lf = jnp.exp(jnp.where(lmask, cf - a_r[h:h + 1, :], NEG_BIG))
                    lb = jnp.exp(jnp.where(umask, e_r[hb_:hb_ + 1, :] - cbk, NEG_BIG))
                    w = lf * dt_r[h:h + 1, :] + lb * dt_r[hb_:hb_ + 1, :]
                    ms.append((cb_g * w).astype(BF16))
                    colf.append(cf)
                    colb.append(a_end[hb_:hb_ + 1, :] - cbk)
                y_diag = jnp.dot(jnp.concatenate(ms, axis=1), rhs, preferred_element_type=F32)
                sel = lane < HEAD_DIM
                rs_f = jnp.exp(jnp.where(sel, colf[0], colf[1]))
                rs_b = jnp.exp(jnp.where(sel, colb[0], colb[1]))
                hs = slice(half * LANES, (half + 1) * LANES)
                y_slab = (y_diag + rs_f * yo_ref[r0:r0 + CHUNK, ks].astype(F32)
                          + rs_b * yob_g[:, hs] + xslab * dskip_ref[:, ks])
                slabs.append(y_slab)
                ha, hb2 = N_HEADS + 2 * k, N_HEADS + 2 * k + 1
                _state_update(st_ref, g, half, bt_g, wb_r[ha:ha + 1, :], wb_r[hb2:hb2 + 1, :], rhs,
                              dec_r[ha:ha + 1, :], dec_r[hb2:hb2 + 1, :], lane)
        y_scr[r0:r0 + CHUNK, :] = jnp.concatenate(slabs, axis=1)

    y = y_scr[...] * _silu(z_ref[...].astype(F32))
    gw = D_SSM // N_GROUPS
    for g in range(N_GROUPS):
        gs = slice(g * gw, (g + 1) * gw)
        yg = y[:, gs]
        yn = yg * lax.rsqrt(jnp.mean(yg * yg, axis=-1, keepdims=True) + EPS) * nw_ref[:, gs]
        o_ref[:, D_SC + g * gw:D_SC + (g + 1) * gw] = yn.astype(BF16)


def _mix_call(proj, u, yo_f, dt, seq_len, scw, a_rep, a_row, tri, trit, dskip, nw):
    t = proj.shape[0]
    nb = t // R_SSD
    bps = seq_len // R_SSD
    hb = R_SSD // HALO_ROWS
    nhalo = t // HALO_ROWS

    def blk(i):
        return nb - 1 - i

    def prev_blk(i):
        return jnp.maximum(blk(i) * hb - 1, 0)

    def next_blk(i):
        return jnp.minimum((blk(i) + 1) * hb, nhalo - 1)

    const = lambda i: (0, 0)
    return pl.pallas_call(
        functools.partial(_mix_body, bps, nb),
        grid=(nb,),
        in_specs=[
            pl.BlockSpec((R_SSD, D_SC), lambda i: (blk(i), 0)),
            pl.BlockSpec((R_SSD, D_SC), lambda i: (blk(i), 1)),
            pl.BlockSpec((R_SSD, D_SC), lambda i: (blk(i), 2)),
            pl.BlockSpec((R_SSD, D_SSM), lambda i: (blk(i), 3)),
            pl.BlockSpec((HALO_ROWS, D_SC), lambda i: (prev_blk(i), 1)),
            pl.BlockSpec((HALO_ROWS, D_SC), lambda i: (next_blk(i), 1)),
            pl.BlockSpec((HALO_ROWS, D_SC), lambda i: (prev_blk(i), 2)),
            pl.BlockSpec((HALO_ROWS, D_SC), lambda i: (next_blk(i), 2)),
            pl.BlockSpec((R_SSD, D_XBC), lambda i: (blk(i), 0)),
            pl.BlockSpec((R_SSD, D_SSM), lambda i: (blk(i), 0)),
            pl.BlockSpec((R_SSD, LANES), lambda i: (blk(i), 0)),
            pl.BlockSpec((SC_CONV, D_SC), const),
            pl.BlockSpec((D_DT, LANES), const),
            pl.BlockSpec((1, LANES), const),
            pl.BlockSpec((CHUNK, CHUNK), const),
            pl.BlockSpec((CHUNK, CHUNK), const),
            pl.BlockSpec((1, D_SSM), const),
            pl.BlockSpec((1, D_SSM), const),
        ],
        out_specs=pl.BlockSpec((R_SSD, D_MODEL), lambda i: (blk(i), 0)),
        out_shape=jax.ShapeDtypeStruct((t, D_MODEL), BF16),
        scratch_shapes=[
            pltpu.VMEM((N_GROUPS, D_STATE, 4 * HEAD_DIM), F32),
            pltpu.VMEM((R_SSD, D_SSM), F32),
        ],
        compiler_params=_params(("arbitrary",)),
        name="mix",
    )(proj, proj, proj, proj, proj, proj, proj, proj, u, yo_f, dt, scw, a_rep, a_row, tri, trit,
      dskip, nw)


def _out_proj_body(y_ref, w_ref, x_ref, nw_ref, g_ref, o_ref):
    for r in range(TM_OUT // OUT_SUB):
        rs = slice(r * OUT_SUB, (r + 1) * OUT_SUB)
        m = jnp.dot(y_ref[rs, :], w_ref[...], preferred_element_type=F32)
        o_ref[rs, :] = x_ref[rs, :] + g_ref[...] * _rms(m, nw_ref[...])


def _out_proj_call(y_mix, x, mod3, seq_of, w_out, nw):
    t = x.shape[0]
    return pl.pallas_call(
        _out_proj_body,
        grid=(t // TM_OUT,),
        in_specs=[
            pl.BlockSpec((TM_OUT, D_MODEL), lambda i: (i, 0)),
            pl.BlockSpec((D_MODEL, D_MODEL), lambda i: (0, 0)),
            pl.BlockSpec((TM_OUT, D_MODEL), lambda i: (i, 0)),
            pl.BlockSpec((1, D_MODEL), lambda i: (0, 0)),
            pl.BlockSpec((None, 1, D_MODEL), lambda i: (seq_of(i, TM_OUT) * 6 + 2, 0, 0)),
        ],
        out_specs=pl.BlockSpec((TM_OUT, D_MODEL), lambda i: (i, 0)),
        out_shape=jax.ShapeDtypeStruct((t, D_MODEL), F32),
        compiler_params=_params(("arbitrary",)),
        name="out_proj",
    )(y_mix, w_out, x, nw, mod3)


def _ffn_body(x_ref, nw1_ref, sh_ref, sc_ref, wg_ref, wu_ref, wd_ref, nw2_ref, g_ref, o_ref, h_scr):
    j = pl.program_id(1)

    @pl.when(j == 0)
    def _():
        h = _rms(x_ref[...], nw1_ref[...]) * (1.0 + sc_ref[...]) + sh_ref[...]
        h_scr[...] = h.astype(BF16)
        o_ref[...] = jnp.zeros_like(o_ref)

    hb = h_scr[...]
    acc = o_ref[...]
    for s in range(TF_FFN // FFN_SUB):
        cs = slice(s * FFN_SUB, (s + 1) * FFN_SUB)
        gate = jnp.dot(hb, wg_ref[:, cs], preferred_element_type=F32)
        up = jnp.dot(hb, wu_ref[:, cs], preferred_element_type=F32)
        act = (_silu(gate) * up).astype(BF16)
        acc = acc + jnp.dot(act, wd_ref[cs, :], preferred_element_type=F32)
    o_ref[...] = acc

    @pl.when(j == pl.num_programs(1) - 1)
    def _():
        o_ref[...] = x_ref[...] + g_ref[...] * _rms(o_ref[...], nw2_ref[...])


def _ffn_call(x1, mod3, seq_of, nw1, wg, wu, wd, nw2):
    t = x1.shape[0]
    return pl.pallas_call(
        _ffn_body,
        grid=(t // TM_FFN, D_FF // TF_FFN),
        in_specs=[
            pl.BlockSpec((TM_FFN, D_MODEL), lambda i, j: (i, 0)),
            pl.BlockSpec((1, D_MODEL), lambda i, j: (0, 0)),
            pl.BlockSpec((None, 1, D_MODEL), lambda i, j: (seq_of(i, TM_FFN) * 6 + 3, 0, 0)),
            pl.BlockSpec((None, 1, D_MODEL), lambda i, j: (seq_of(i, TM_FFN) * 6 + 4, 0, 0)),
            pl.BlockSpec((D_MODEL, TF_FFN), lambda i, j: (0, j)),
            pl.BlockSpec((D_MODEL, TF_FFN), lambda i, j: (0, j)),
            pl.BlockSpec((TF_FFN, D_MODEL), lambda i, j: (j, 0)),
            pl.BlockSpec((1, D_MODEL), lambda i, j: (0, 0)),
            pl.BlockSpec((None, 1, D_MODEL), lambda i, j: (seq_of(i, TM_FFN) * 6 + 5, 0, 0)),
        ],
        out_specs=pl.BlockSpec((TM_FFN, D_MODEL), lambda i, j: (i, 0)),
        out_shape=jax.ShapeDtypeStruct((t, D_MODEL), F32),
        scratch_shapes=[pltpu.VMEM((TM_FFN, D_MODEL), BF16)],
        compiler_params=_params(("arbitrary", "arbitrary")),
        name="ffn",
    )(x1, nw1, mod3, mod3, wg, wu, wd, nw2, mod3)


def _layer(x, mod3, seq_base, seq_len, p):
    def seq_of(i, tm):
        return seq_base + (i * tm) // seq_len

    proj, dt = _in_proj_call(x, mod3, seq_of, p["norm_pre_mix"], p["w_in_main"], p["w_in_dt"],
                             p["bias_row"])
    u, yo_f = _ssd_fwd_call(proj, dt, seq_len, p["ssm_conv_w"], p["ssm_conv_b"], p["a_rep"],
                            p["trit"])
    y_mix = _mix_call(proj, u, yo_f, dt, seq_len, p["sc_conv_w"], p["a_rep"], p["a_row"],
                      p["tri"], p["trit"], p["dskip"], p["ssm_norm_w"])
    x1 = _out_proj_call(y_mix, x, mod3, seq_of, p["w_out"], p["norm_post_mix"])
    return _ffn_call(x1, mod3, seq_of, p["norm_pre_ffn"], p["w_gate"], p["w_up"], p["w_down"],
                     p["norm_post_ffn"])


def _prep_layer(i, w_ada, b_ada, norm_pre_mix, norm_post_mix, w_in, sc_conv_w, ssm_conv_w,
                ssm_conv_b, dt_bias_f, dt_bias_b, a_log_f, a_log_b, d_skip, ssm_norm_w, w_out,
                norm_pre_ffn, norm_post_ffn, w_gate, w_up, w_down):
    row = lambda v: v[i].astype(F32).reshape(1, -1)
    w_in_i = w_in[i]
    w_dt = jnp.pad(w_in_i[:, D_MAIN:], ((0, 0), (0, LANES - D_DT))).astype(BF16)
    dt_bias = jnp.concatenate([dt_bias_f[i], dt_bias_b[i]]).astype(F32)
    a = -jnp.exp(jnp.concatenate([a_log_f[i], a_log_b[i]]).astype(F32))
    pad_row = lambda v: jnp.pad(v, (0, LANES - D_DT)).reshape(1, LANES)
    tri = jnp.tril(jnp.ones((CHUNK, CHUNK), F32))
    return {
        "w_ada": w_ada[i], "b_ada": row(b_ada),
        "norm_pre_mix": row(norm_pre_mix), "norm_post_mix": row(norm_post_mix),
        "w_in_main": w_in_i[:, :D_MAIN].astype(BF16), "w_in_dt": w_dt,
        "sc_conv_w": sc_conv_w[i].astype(F32), "ssm_conv_w": ssm_conv_w[i].astype(F32),
        "ssm_conv_b": row(ssm_conv_b),
        "a_rep": jnp.broadcast_to(a[:, None], (D_DT, LANES)),
        "bias_row": pad_row(dt_bias), "a_row": pad_row(a),
        "tri": tri, "trit": tri.T,
        "dskip": jnp.repeat(d_skip[i].astype(F32), HEAD_DIM).reshape(1, D_SSM),
        "ssm_norm_w": row(ssm_norm_w),
        "w_out": w_out[i].astype(BF16),
        "norm_pre_ffn": row(norm_pre_ffn), "norm_post_ffn": row(norm_post_ffn),
        "w_gate": w_gate[i].astype(BF16), "w_up": w_up[i].astype(BF16),
        "w_down": w_down[i].astype(BF16),
    }


def kernel(x_prompt, x_sample, c_prompt, c_sample, w_ada, b_ada, norm_pre_mix, norm_post_mix, w_in,
           sc_conv_w, ssm_conv_w, ssm_conv_b, dt_bias_f, dt_bias_b, a_log_f, a_log_b, d_skip,
           ssm_norm_w, w_out, norm_pre_ffn, norm_post_ffn, w_gate, w_up, w_down):
    bp, lp, d = x_prompt.shape
    bs, ls, _ = x_sample.shape
    assert d == D_MODEL and bp + bs <= MOD_ROWS
    for n in (lp, ls):
        assert n % max(TM_IN, R_SSD, TM_OUT, TM_FFN) == 0
    yp = x_prompt.reshape(bp * lp, d)
    ys = x_sample.reshape(bs * ls, d)
    c_all = jnp.concatenate(
        [c_prompt, c_sample, jnp.zeros((MOD_ROWS - bp - bs, d), c_prompt.dtype)], axis=0).astype(F32)
    for i in range(w_ada.shape[0]):
        p = _prep_layer(i, w_ada, b_ada, norm_pre_mix, norm_post_mix, w_in, sc_conv_w, ssm_conv_w,
                        ssm_conv_b, dt_bias_f, dt_bias_b, a_log_f, a_log_b, d_skip, ssm_norm_w,
                        w_out, norm_pre_ffn, norm_post_ffn, w_gate, w_up, w_down)
        mod = _mod_call(c_all, p["w_ada"], p["b_ada"])
        mod3 = mod.reshape(MOD_ROWS * 6, 1, D_MODEL)
        yp = _layer(yp, mod3, 0, lp, p)
        ys = _layer(ys, mod3, bp, ls, p)
    return yp.reshape(x_prompt.shape), ys.reshape(x_sample.shape)
```
